```python
import jax, jax.numpy as jnp
from jax import lax
import numpy as np

D_MODEL = 2048
BATCH = 2
SEQ = 16384
DEPTH = 1

CHUNK = 64
EPS = 1e-6

GLA_HEADS = 4
GLA_WIDTH = D_MODEL // 2
GLA_DV = GLA_WIDTH // GLA_HEADS
GLA_DK = GLA_DV // 2
GLA_QK = GLA_HEADS * GLA_DK
GLA_GATE_RANK = 16
GLA_TAU = 16.0

MLSTM_HEADS = 4
MLSTM_WIDTH = D_MODEL - GLA_WIDTH
MLSTM_DV = MLSTM_WIDTH // MLSTM_HEADS
MLSTM_DK = MLSTM_DV // 2
MLSTM_QK = MLSTM_HEADS * MLSTM_DK
CONV_WIDTH = 4

IN_SPLITS = (GLA_QK, GLA_QK, GLA_WIDTH, GLA_WIDTH, GLA_GATE_RANK,
             MLSTM_QK, MLSTM_QK, MLSTM_WIDTH, MLSTM_WIDTH, MLSTM_HEADS, MLSTM_HEADS)
IN_COLS = sum(IN_SPLITS)

PEER_HEADS = 8
N_KEYS = 128
N_EXPERTS = N_KEYS * N_KEYS
PEER_QDIM = 256
PEER_TOPK = 16
PEER_BLOCK = 128

PLE_DIM = 256

kernel_name = "hybrid_gla_mlstm_peer_ple"


def rmsnorm(x, g):
    xf = x.astype(jnp.float32)
    y = xf * lax.rsqrt(jnp.mean(xf * xf, axis=-1, keepdims=True) + EPS)
    return (y * g.astype(jnp.float32)).astype(x.dtype)


def head_rmsnorm(h, g, n_heads):
    b, s, w = h.shape
    hh = h.reshape(b, s, n_heads, w // n_heads)
    hh = hh * lax.rsqrt(jnp.mean(hh * hh, axis=-1, keepdims=True) + EPS)
    return hh.reshape(b, s, w) * g.astype(jnp.float32)


def split_cols(proj):
    out, start = [], 0
    for n in IN_SPLITS:
        out.append(proj[..., start:start + n])
        start += n
    return out


def to_chunks(t, n_heads):
    b, s, w = t.shape
    t = t.reshape(b, s // CHUNK, CHUNK, n_heads, w // n_heads)
    return t.transpose(1, 0, 3, 2, 4).astype(jnp.float32)


def from_chunks(t):
    nc, b, h, l, d = t.shape
    return t.transpose(1, 0, 3, 2, 4).reshape(b, nc * l, h * d)


def gates_to_chunks(t):
    b, s, h = t.shape
    return t.reshape(b, s // CHUNK, CHUNK, h).transpose(1, 0, 3, 2).astype(jnp.float32)


def causal_dwconv(x, w, bias):
    out = lax.conv_general_dilated(
        x, w[:, None, :].astype(x.dtype), window_strides=(1,),
        padding=[(CONV_WIDTH - 1, 0)],
        dimension_numbers=('NWC', 'WIO', 'NWC'),
        feature_group_count=x.shape[-1])
    return out + bias.astype(x.dtype)


def gla_chunked(q, k, v, log_a):
    mask = jnp.tril(jnp.ones((CHUNK, CHUNK), dtype=bool))
    nb, nh = q.shape[1], q.shape[2]

    def step(state, inp):
        qc, kc, vc, gc = inp
        b = jnp.cumsum(gc, axis=2)
        diff = b[:, :, :, None, :] - b[:, :, None, :, :]
        decay = jnp.exp(jnp.where(mask[:, :, None], diff, -jnp.inf))
        scores = jnp.sum(qc[:, :, :, None, :] * kc[:, :, None, :, :] * decay, axis=-1)
        o = scores @ vc + jnp.einsum('bhid,bhdv->bhiv', qc * jnp.exp(b), state)
        b_last = b[:, :, -1, :]
        k_dec = kc * jnp.exp(b_last[:, :, None, :] - b)
        state = jnp.exp(b_last)[..., None] * state + jnp.einsum('bhjd,bhjv->bhdv', k_dec, vc)
        return state, o

    s0 = jnp.zeros((nb, nh, GLA_DK, GLA_DV), jnp.float32)
    _, o = lax.scan(step, s0, (q, k, v, log_a))
    return o


def mlstm_chunked(q, k, v, ig, logf):
    mask = jnp.tril(jnp.ones((CHUNK, CHUNK), dtype=bool))
    nb, nh = q.shape[1], q.shape[2]

    def step(carry, inp):
        c_st, n_st, m_st = carry
        qc, kc, vc, ic, fc = inp
        f_cum = jnp.cumsum(fc, axis=-1)
        dmat = jnp.where(mask, f_cum[..., :, None] - f_cum[..., None, :] + ic[..., None, :], -jnp.inf)
        inter = f_cum + m_st[..., None]
        m_row = jnp.maximum(jnp.max(dmat, axis=-1), inter)
        w = jnp.exp(dmat - m_row[..., None])
        s = jnp.einsum('bhid,bhjd->bhij', qc, kc) * w
        w_inter = jnp.exp(inter - m_row)
        num = s @ vc + w_inter[..., None] * jnp.einsum('bhid,bhdv->bhiv', qc, c_st)
        den = jnp.sum(s, axis=-1) + w_inter * jnp.einsum('bhid,bhd->bhi', qc, n_st)
        h = num / jnp.maximum(jnp.abs(den), jnp.exp(-m_row))[..., None]
        f_last = f_cum[..., -1]
        wlog = f_last[..., None] - f_cum + ic
        m_new = jnp.maximum(f_last + m_st, jnp.max(wlog, axis=-1))
        wk = jnp.exp(wlog - m_new[..., None])
        carry_dec = jnp.exp(f_last + m_st - m_new)
        c_st = carry_dec[..., None, None] * c_st + jnp.einsum('bhj,bhjd,bhjv->bhdv', wk, kc, vc)
        n_st = carry_dec[..., None] * n_st + jnp.einsum('bhj,bhjd->bhd', wk, kc)
        return (c_st, n_st, m_new), h

    init = (jnp.zeros((nb, nh, MLSTM_DK, MLSTM_DV), jnp.float32),
            jnp.zeros((nb, nh, MLSTM_DK), jnp.float32),
            jnp.zeros((nb, nh), jnp.float32))
    _, h = lax.scan(step, init, (q, k, v, ig, logf))
    return h


def token_mixers(xn, w_in, gla_gate_up, gla_gate_bias, gla_norm, conv_w, conv_b,
                 igate_bias, fgate_bias, mlstm_norm, w_out):
    gq, gk, gv, gg, glr, mq, mk, mv, mo, mi, mf = split_cols(xn @ w_in)
    log_a = jax.nn.log_sigmoid((glr @ gla_gate_up + gla_gate_bias).astype(jnp.float32)) / GLA_TAU
    o_gla = gla_chunked(to_chunks(gq, GLA_HEADS) * (GLA_DK ** -0.5), to_chunks(gk, GLA_HEADS),
                        to_chunks(gv, GLA_HEADS), to_chunks(log_a, GLA_HEADS))
    y_gla = head_rmsnorm(from_chunks(o_gla), gla_norm, GLA_HEADS) * jax.nn.silu(gg.astype(jnp.float32))
    qk = jax.nn.silu(causal_dwconv(jnp.concatenate([mq, mk], axis=-1), conv_w, conv_b))
    mq_c, mk_c = qk[..., :MLSTM_QK], qk[..., MLSTM_QK:]
    ig = gates_to_chunks(mi + igate_bias)
    logf = jax.nn.log_sigmoid(gates_to_chunks(mf + fgate_bias))
    h = mlstm_chunked(to_chunks(mq_c, MLSTM_HEADS), to_chunks(mk_c, MLSTM_HEADS) * (MLSTM_DK ** -0.5),
                      to_chunks(mv, MLSTM_HEADS), ig, logf)
    y_m = jax.nn.sigmoid(mo.astype(jnp.float32)) * head_rmsnorm(from_chunks(h), mlstm_norm, MLSTM_HEADS)
    y = jnp.concatenate([y_gla, y_m], axis=-1).astype(xn.dtype)
    return y @ w_out


def peer_ffn(xn, w_query, sub_keys, expert_u, expert_v):
    b, s, d = xn.shape
    xt = xn.reshape(-1, PEER_BLOCK, d)

    def block(xb):
        q = (xb @ w_query).reshape(PEER_BLOCK, PEER_HEADS, 2, PEER_QDIM // 2).astype(jnp.float32)
        sc = jnp.einsum('thpd,hpkd->thpk', q, sub_keys.astype(jnp.float32))
        sv, si = lax.top_k(sc, PEER_TOPK)
        cand = (sv[:, :, 0, :, None] + sv[:, :, 1, None, :]).reshape(PEER_BLOCK, PEER_HEADS, PEER_TOPK * PEER_TOPK)
        cidx = (si[:, :, 0, :, None] * N_KEYS + si[:, :, 1, None, :]).reshape(PEER_BLOCK, PEER_HEADS, PEER_TOPK * PEER_TOPK)
        top_s, pos = lax.top_k(cand, PEER_TOPK)
        idx = jnp.take_along_axis(cidx, pos, axis=-1)
        g = jax.nn.softmax(top_s, axis=-1)
        a = jax.nn.gelu(jnp.einsum('thkd,td->thk', expert_u[idx], xb).astype(jnp.float32))
        return jnp.einsum('thk,thkd->td', (g * a).astype(xb.dtype), expert_v[idx])

    return lax.map(block, xt).reshape(b, s, d)


def setup_inputs(seed: int = 0) -> dict:
    key = jax.random.key(seed)
    ks = jax.random.split(key, 24)
    f32 = jnp.float32

    def nrm(k, shape, scale):
        return jax.random.normal(k, shape, f32) * scale

    def gain(k, shape):
        return 1.0 + 0.02 * jax.random.normal(k, shape, f32)

    return {
        "x": nrm(ks[0], (BATCH, SEQ, D_MODEL), 1.0),
        "p": nrm(ks[1], (DEPTH, BATCH, SEQ, PLE_DIM), 1.0),
        "norm_mix": gain(ks[2], (DEPTH, D_MODEL)),
        "w_in": nrm(ks[3], (DEPTH, D_MODEL, IN_COLS), D_MODEL ** -0.5),
        "gla_gate_up": nrm(ks[4], (DEPTH, GLA_GATE_RANK, GLA_QK), GLA_GATE_RANK ** -0.5),
        "gla_gate_bias": nrm(ks[5], (DEPTH, GLA_QK), 0.1),
        "gla_norm": gain(ks[6], (DEPTH, GLA_WIDTH)),
        "mlstm_conv_w": nrm(ks[7], (DEPTH, CONV_WIDTH, 2 * MLSTM_QK), CONV_WIDTH ** -0.5),
        "mlstm_conv_b": nrm(ks[8], (DEPTH, 2 * MLSTM_QK), 0.02),
        "mlstm_igate_bias": nrm(ks[9], (DEPTH, MLSTM_HEADS), 0.1),
        "mlstm_fgate_bias": 3.0 + 3.0 * jax.random.uniform(ks[10], (DEPTH, MLSTM_HEADS), f32),
        "mlstm_norm": gain(ks[11], (DEPTH, MLSTM_WIDTH)),
        "w_out": nrm(ks[12], (DEPTH, GLA_WIDTH + MLSTM_WIDTH, D_MODEL), (GLA_WIDTH + MLSTM_WIDTH) ** -0.5),
        "norm_ffn": gain(ks[13], (DEPTH, D_MODEL)),
        "peer_query": nrm(ks[14], (DEPTH, D_MODEL, PEER_HEADS * PEER_QDIM), D_MODEL ** -0.5),
        "peer_sub_keys": nrm(ks[15], (DEPTH, PEER_HEADS, 2, N_KEYS, PEER_QDIM // 2), (PEER_QDIM // 2) ** -0.5),
        "peer_u": nrm(ks[16], (DEPTH, N_EXPERTS, D_MODEL), D_MODEL ** -0.5),
        "peer_v": nrm(ks[17], (DEPTH, N_EXPERTS, D_MODEL), 0.5),
        "norm_ple": gain(ks[18], (DEPTH, D_MODEL)),
        "w_ple": nrm(ks[19], (DEPTH, PLE_DIM, D_MODEL), PLE_DIM ** -0.5),
        "w_ple_gate": nrm(ks[20], (DEPTH, D_MODEL, D_MODEL), D_MODEL ** -0.5),
        "norm_final": gain(ks[21], (D_MODEL,)),
    }


def reference(x, p, norm_mix, w_in, gla_gate_up, gla_gate_bias, gla_norm, mlstm_conv_w,
              mlstm_conv_b, mlstm_igate_bias, mlstm_fgate_bias, mlstm_norm, w_out, norm_ffn,
              peer_query, peer_sub_keys, peer_u, peer_v, norm_ple, w_ple, w_ple_gate, norm_final):
    h = x
    for i in range(DEPTH):
        h = h + token_mixers(rmsnorm(h, norm_mix[i]), w_in[i], gla_gate_up[i], gla_gate_bias[i],
                             gla_norm[i], mlstm_conv_w[i], mlstm_conv_b[i], mlstm_igate_bias[i],
                             mlstm_fgate_bias[i], mlstm_norm[i], w_out[i])
        h = h + peer_ffn(rmsnorm(h, norm_ffn[i]), peer_query[i], peer_sub_keys[i], peer_u[i], peer_v[i])
        gate = jax.nn.sigmoid(rmsnorm(h, norm_ple[i]) @ w_ple_gate[i])
        h = h + gate * (p[i] @ w_ple[i])
    return rmsnorm(h, norm_final)
```

```python
import functools

import numpy as np
import jax
import jax.numpy as jnp
from jax import lax
from jax.experimental import pallas as pl
from jax.experimental.pallas import tpu as pltpu

F32 = jnp.float32
BF16 = jnp.bfloat16
I32 = jnp.int32

EPS = 1e-6
CHUNK = 64
GLA_TAU = 16.0
HEADS = 4
DK = 128
DV = 256
QK_SCALE = DK ** -0.5
GATE_RANK = 16
CONV_WIDTH = 4
PEER_HEADS = 8
N_KEYS = 128
PEER_TOPK = 16
PEER_PAIRS = PEER_HEADS * PEER_TOPK

VMEM_LIMIT = 48 * 1024 * 1024

BIG_COLS = 6144
COL_GQ, COL_GK, COL_GV, COL_GG = 0, 512, 1024, 2048
COL_MQ, COL_MK, COL_MV, COL_MO = 3072, 3584, 4096, 5120
SM_GLR, SM_MI, SM_MF = 0, 16, 20


def _log_sigmoid(z):
    return jnp.minimum(z, 0.0) - jnp.log1p(jnp.exp(-jnp.abs(z)))


def _sigmoid(z):
    return 1.0 / (1.0 + jnp.exp(-z))


def _rms(x, g):
    return x * lax.rsqrt(jnp.mean(x * x, axis=-1, keepdims=True) + EPS) * g


def _split3(g):
    g1 = g.astype(BF16)
    r1 = g - g1.astype(F32)
    g2 = r1.astype(BF16)
    r2 = r1 - g2.astype(F32)
    return g1, g2, r2.astype(BF16)


def _nt(a, b):
    return lax.dot_general(a, b, (((1,), (1,)), ((), ())), preferred_element_type=F32)


def _tn(a, b):
    return lax.dot_general(a, b, (((0,), (0,)), ((), ())), preferred_element_type=F32)


def _inproj_kernel(x_ref, nw_ref, wbig_ref, wsm_ref, wsmt_ref, big_ref, sm_ref, smt_ref, xn_ref):
    @pl.when(pl.program_id(1) == 0)
    def _():
        xn = _rms(x_ref[...], nw_ref[...]).astype(BF16)
        xn_ref[...] = xn
        sm_ref[...] = jnp.dot(xn, wsm_ref[...], preferred_element_type=F32)
        smt_ref[...] = _nt(wsmt_ref[...], xn)

    big_ref[...] = jnp.dot(xn_ref[...], wbig_ref[...], preferred_element_type=F32).astype(BF16)


def _inproj(x2, nw, wbig, wsm, wsmt, tm=512, tn=1536):
    t, d = x2.shape
    grid = (t // tm, BIG_COLS // tn)
    return pl.pallas_call(
        _inproj_kernel,
        grid=grid,
        in_specs=[
            pl.BlockSpec((tm, d), lambda i, j: (i, 0)),
            pl.BlockSpec((1, d), lambda i, j: (0, 0)),
            pl.BlockSpec((d, tn), lambda i, j: (0, j)),
            pl.BlockSpec((d, 128), lambda i, j: (0, 0)),
            pl.BlockSpec((128, d), lambda i, j: (0, 0)),
        ],
        out_specs=[
            pl.BlockSpec((tm, tn), lambda i, j: (i, j)),
            pl.BlockSpec((tm, 128), lambda i, j: (i, 0)),
            pl.BlockSpec((128, tm), lambda i, j: (0, i)),
        ],
        out_shape=[
            jax.ShapeDtypeStruct((t, BIG_COLS), BF16),
            jax.ShapeDtypeStruct((t, 128), F32),
            jax.ShapeDtypeStruct((128, t), F32),
        ],
        scratch_shapes=[pltpu.VMEM((tm, d), BF16)],
        compiler_params=pltpu.CompilerParams(
            dimension_semantics=("parallel", "arbitrary"), vmem_limit_bytes=VMEM_LIMIT),
        name="inproj",
    )(x2, nw, wbig, wsm, wsmt)


def _gla_consts():
    c = CHUNK
    r = np.arange(c)
    mats = [(r[None, :] <= r[:, None]), (r[None, :] > r[:, None])]
    masks = []
    for lvl in range(6):
        m = 32 >> lvl
        blk = r // (2 * m)
        second = (r % (2 * m)) >= m
        mid = blk * 2 * m + m - 1
        mq = second[:, None] & (r[None, :] > mid[:, None]) & (r[None, :] <= r[:, None])
        mk = (~second)[:, None] & (r[None, :] > r[:, None]) & (r[None, :] <= mid[:, None])
        mats.append(mq | mk)
        masks.append(second[:, None] & (~second)[None, :] & (blk[:, None] == blk[None, :]))
    return (np.concatenate(mats, 0).astype(np.float32),
            np.stack(masks, 0).astype(np.float32))


def _gla_kernel(q_ref, k_ref, v_ref, gg_ref, sm_ref, gu_ref, gb_ref, gn_ref, mc_ref, lm_ref,
                y_ref, s_ref, la_ref):
    ls = q_ref.shape[0]

    @pl.when(pl.program_id(2) == 0)
    def _():
        s_ref[...] = jnp.zeros_like(s_ref)

    z = jnp.dot(sm_ref[...].astype(BF16), gu_ref[0], preferred_element_type=F32) + gb_ref[0]
    la_ref[...] = _log_sigmoid(z) * (1.0 / GLA_TAU)

    rows = lax.broadcasted_iota(I32, (CHUNK, CHUNK), 0)
    cols = lax.broadcasted_iota(I32, (CHUNK, CHUNK), 1)
    eye = (rows == cols).astype(F32)
    ones_bf = jnp.ones((CHUNK, DK), BF16)

    def chunk(c, carry):
        r0 = pl.multiple_of(c * CHUNK, CHUNK)
        g = la_ref[pl.ds(r0, CHUNK), :]
        g3 = jnp.concatenate(_split3(g), axis=1)
        xx = jnp.dot(mc_ref[...], g3, preferred_element_type=F32)
        xx = xx[:, :DK] + xx[:, DK:2 * DK] + xx[:, 2 * DK:]
        tot = _tn(g3, ones_bf)
        tot = tot[:DK] + tot[DK:2 * DK] + tot[2 * DK:]

        qf = q_ref[pl.ds(r0, CHUNK), :].astype(F32) * QK_SCALE
        kf = k_ref[pl.ds(r0, CHUNK), :].astype(F32)
        v = v_ref[pl.ds(r0, CHUNK), :]
        state = s_ref[...]

        o = jnp.dot((qf * jnp.exp(xx[0:CHUNK])).astype(BF16), state.astype(BF16),
                    preferred_element_type=F32)
        sc = jnp.sum(qf * kf, axis=-1, keepdims=True) * eye
        for lvl in range(6):
            f = jnp.exp(xx[128 + CHUNK * lvl:128 + CHUNK * (lvl + 1)])
            sc = sc + _nt((qf * f).astype(BF16), (kf * f).astype(BF16)) * lm_ref[lvl]
        o = o + jnp.dot(sc.astype(BF16), v, preferred_element_type=F32)

        kd = (kf * jnp.exp(xx[CHUNK:2 * CHUNK])).astype(BF16)
        dec = jnp.exp(tot)
        s_ref[...] = state * jnp.concatenate([dec, dec], axis=1) + _tn(kd, v)

        gg = gg_ref[pl.ds(r0, CHUNK), :].astype(F32)
        y = _rms(o, gn_ref[0]) * (gg * _sigmoid(gg))
        y_ref[pl.ds(r0, CHUNK), :] = y.astype(BF16)
        return carry

    lax.fori_loop(0, ls // CHUNK, chunk, 0)


def _gla(big, sm, gu, gb, gn, mconst, lmask, batch, seq, ls=512):
    t = big.shape[0]
    ns = seq // ls
    row = lambda b, h, s: b * ns + s
    return pl.pallas_call(
        _gla_kernel,
        grid=(batch, HEADS, ns),
        in_specs=[
            pl.BlockSpec((ls, DK), lambda b, h, s: (row(b, h, s), COL_GQ // DK + h)),
            pl.BlockSpec((ls, DK), lambda b, h, s: (row(b, h, s), COL_GK // DK + h)),
            pl.BlockSpec((ls, DV), lambda b, h, s: (row(b, h, s), COL_GV // DV + h)),
            pl.BlockSpec((ls, DV), lambda b, h, s: (row(b, h, s), COL_GG // DV + h)),
            pl.BlockSpec((ls, 128), lambda b, h, s: (row(b, h, s), 0)),
            pl.BlockSpec((1, 128, DK), lambda b, h, s: (h, 0, 0)),
            pl.BlockSpec((1, 1, DK), lambda b, h, s: (h, 0, 0)),
            pl.BlockSpec((1, 1, DV), lambda b, h, s: (h, 0, 0)),
            pl.BlockSpec((8 * CHUNK, CHUNK), lambda b, h, s: (0, 0)),
            pl.BlockSpec((6, CHUNK, CHUNK), lambda b, h, s: (0, 0, 0)),
        ],
        out_specs=pl.BlockSpec((ls, DV), lambda b, h, s: (row(b, h, s), h)),
        out_shape=jax.ShapeDtypeStruct((t, HEADS * DV), BF16),
        scratch_shapes=[pltpu.VMEM((DK, DV), F32), pltpu.VMEM((ls, DK), F32)],
        compiler_params=pltpu.CompilerParams(
            dimension_semantics=("parallel", "parallel", "arbitrary"),
            vmem_limit_bytes=VMEM_LIMIT),
        name="gla",
    )(big, big, big, big, sm, gu, gb, gn, mconst, lmask)


def _mlstm_kernel(q_ref, k_ref, v_ref, mo_ref, sm_ref, smt_ref, cwq_ref, cwk_ref, cbq_ref,
                  cbk_ref, gbc_ref, gbr_ref, mn_ref, tri_ref, trit_ref,
                  y_ref, c_ref, m_ref, qx_ref, kx_ref, qc_ref, kc_ref, fc_ref, ft_ref):
    ls = q_ref.shape[0]
    h = pl.program_id(1)

    @pl.when(pl.program_id(2) == 0)
    def _():
        c_ref[...] = jnp.zeros_like(c_ref)
        m_ref[...] = jnp.zeros_like(m_ref)
        qx_ref[0:8, :] = jnp.zeros((8, DK), F32)
        kx_ref[0:8, :] = jnp.zeros((8, DK), F32)

    qx_ref[8:, :] = q_ref[...].astype(F32)
    kx_ref[8:, :] = k_ref[...].astype(F32)

    def conv(x_ref, w_ref, b_ref):
        acc = b_ref[...] + w_ref[0:1, :] * x_ref[5:5 + ls, :]
        for kk in range(1, CONV_WIDTH):
            acc = acc + w_ref[kk:kk + 1, :] * x_ref[5 + kk:5 + kk + ls, :]
        return acc * _sigmoid(acc)

    qc_ref[...] = conv(qx_ref, cwq_ref, cbq_ref)
    kc_ref[...] = conv(kx_ref, cwk_ref, cbk_ref) * QK_SCALE
    qx_ref[0:8, :] = qx_ref[ls:ls + 8, :]
    kx_ref[0:8, :] = kx_ref[ls:ls + 8, :]

    lane = lax.broadcasted_iota(I32, (ls, 128), 1)
    gc = sm_ref[...] + gbc_ref[...]
    fc_ref[...] = jnp.where(lane >= SM_MF, _log_sigmoid(gc), gc)
    sub = lax.broadcasted_iota(I32, (8, ls), 0)
    gr = smt_ref[...] + gbr_ref[...]
    gr = jnp.where(sub >= HEADS, _log_sigmoid(gr), gr)
    for c in range(ls // CHUNK):
        ft_ref[c] = gr[:, c * CHUNK:(c + 1) * CHUNK]

    rows = lax.broadcasted_iota(I32, (CHUNK, CHUNK), 0)
    cols = lax.broadcasted_iota(I32, (CHUNK, CHUNK), 1)
    causal = cols <= rows
    lane_c = lax.broadcasted_iota(I32, (CHUNK, 128), 1)
    sub_c = lax.broadcasted_iota(I32, (8, CHUNK), 0)
    vlane = lax.broadcasted_iota(I32, (CHUNK, 128), 1)
    one_col = jnp.where(vlane == 0, 1.0, 0.0).astype(BF16)

    def pick_col(a, idx):
        return jnp.sum(jnp.where(lane_c == idx, a, 0.0), axis=1, keepdims=True)

    def pick_row(a, idx):
        return jnp.sum(jnp.where(sub_c == idx, a, 0.0), axis=0, keepdims=True)

    def chunk(c, carry):
        r0 = pl.multiple_of(c * CHUNK, CHUNK)
        gcol = fc_ref[pl.ds(r0, CHUNK), :]
        grow = ft_ref[c]
        cum_c = jnp.dot(tri_ref[...], jnp.concatenate(_split3(gcol), axis=1),
                        preferred_element_type=F32)
        cum_c = cum_c[:, :128] + cum_c[:, 128:256] + cum_c[:, 256:]
        g1, g2, g3 = _split3(grow)
        cum_r = (jnp.dot(g1, trit_ref[...], preferred_element_type=F32)
                 + jnp.dot(g2, trit_ref[...], preferred_element_type=F32)
                 + jnp.dot(g3, trit_ref[...], preferred_element_type=F32))
        f_col = pick_col(cum_c, SM_MF + h)
        i_col = pick_col(gcol, SM_MI + h)
        f_row = pick_row(cum_r, HEADS + h)
        i_row = pick_row(grow, h)
        m_prev = m_ref[0:1, 0:1]

        q = qc_ref[pl.ds(r0, CHUNK), :]
        k = kc_ref[pl.ds(r0, CHUNK), :]
        qb = q.astype(BF16)
        vaug = jnp.concatenate([v_ref[pl.ds(r0, CHUNK), :], one_col], axis=1)

        dmat = jnp.where(causal, f_col - f_row + i_row, -jnp.inf)
        inter = f_col + m_prev
        m_row = jnp.maximum(jnp.max(dmat, axis=1, keepdims=True), inter)
        s = _nt(qb, k.astype(BF16)) * jnp.exp(dmat - m_row)
        w_inter = jnp.exp(inter - m_row)
        cst = c_ref[...]
        comb = (jnp.dot(s.astype(BF16), vaug, preferred_element_type=F32)
                + w_inter * jnp.dot(qb, cst.astype(BF16), preferred_element_type=F32))
        num = comb[:, :DV]
        den = comb[:, DV:DV + 1]
        hh = num / jnp.maximum(jnp.abs(den), jnp.exp(-m_row))

        f_last = f_col[CHUNK - 1:CHUNK, :]
        wlog = f_last - f_col + i_col
        m_new = jnp.maximum(f_last + m_prev, jnp.max(wlog, axis=0, keepdims=True))
        wk = jnp.exp(wlog - m_new)
        c_ref[...] = jnp.exp(f_last + m_prev - m_new) * cst + _tn((wk * k).astype(BF16), vaug)
        m_ref[...] = jnp.broadcast_to(m_new, m_ref.shape)

        mo = mo_ref[pl.ds(r0, CHUNK), :].astype(F32)
        y_ref[pl.ds(r0, CHUNK), :] = (_sigmoid(mo) * _rms(hh, mn_ref[0])).astype(BF16)
        return carry

    lax.fori_loop(0, ls // CHUNK, chunk, 0)


def _mlstm(big, sm, smt, cw, cb, gbc, gbr, mn, tri, trit, batch, seq, ls=512):
    t = big.shape[0]
    ns = seq // ls
    row = lambda b, h, s: b * ns + s
    return pl.pallas_call(
        _mlstm_kernel,
        grid=(batch, HEADS, ns),
        in_specs=[
            pl.BlockSpec((ls, DK), lambda b, h, s: (row(b, h, s), COL_MQ // DK + h)),
            pl.BlockSpec((ls, DK), lambda b, h, s: (row(b, h, s), COL_MK // DK + h)),
            pl.BlockSpec((ls, DV), lambda b, h, s: (row(b, h, s), COL_MV // DV + h)),
            pl.BlockSpec((ls, DV), lambda b, h, s: (row(b, h, s), COL_MO // DV + h)),
            pl.BlockSpec((ls, 128), lambda b, h, s: (row(b, h, s), 0)),
            pl.BlockSpec((8, ls), lambda b, h, s: (SM_MI // 8, row(b, h, s))),
            pl.BlockSpec((CONV_WIDTH, DK), lambda b, h, s: (0, h)),
            pl.BlockSpec((CONV_WIDTH, DK), lambda b, h, s: (0, HEADS + h)),
            pl.BlockSpec((1, DK), lambda b, h, s: (0, h)),
            pl.BlockSpec((1, DK), lambda b, h, s: (0, HEADS + h)),
            pl.BlockSpec((1, 128), lambda b, h, s: (0, 0)),
            pl.BlockSpec((8, 1), lambda b, h, s: (0, 0)),
            pl.BlockSpec((1, 1, DV), lambda b, h, s: (h, 0, 0)),
            pl.BlockSpec((CHUNK, CHUNK), lambda b, h, s: (0, 0)),
            pl.BlockSpec((CHUNK, CHUNK), lambda b, h, s: (0, 0)),
        ],
        out_specs=pl.BlockSpec((ls, DV), lambda b, h, s: (row(b, h, s), h)),
        out_shape=jax.ShapeDtypeStruct((t, HEADS * DV), BF16),
        scratch_shapes=[
            pltpu.VMEM((DK, DV + 128), F32),
            pltpu.VMEM((8, 128), F32),
            pltpu.VMEM((ls + 8, DK), F32),
            pltpu.VMEM((ls + 8, DK), F32),
            pltpu.VMEM((ls, DK), F32),
            pltpu.VMEM((ls, DK), F32),
            pltpu.VMEM((ls, 128), F32),
            pltpu.VMEM((ls // CHUNK, 8, CHUNK), F32),
        ],
        compiler_params=pltpu.CompilerParams(
            dimension_semantics=("parallel", "parallel", "arbitrary"),
            vmem_limit_bytes=VMEM_LIMIT),
        name="mlstm",
    )(big, big, big, big, sm, smt, cw, cw, cb, cb, gbc, gbr, mn, tri, trit)


def _outproj_kernel(yg_ref, ym_ref, x_ref, wo_ref, nf_ref, h_ref, xn_ref):
    half = yg_ref.shape[1]
    acc = jnp.dot(yg_ref[...], wo_ref[0:half, :], preferred_element_type=F32)
    acc = acc + jnp.dot(ym_ref[...], wo_ref[half:, :], preferred_element_type=F32)
    hh = x_ref[...] + acc
    h_ref[...] = hh
    xn_ref[...] = _rms(hh, nf_ref[...]).astype(BF16)


def _outproj(yg, ym, x2, wo, nf, tm=512):
    t, d = x2.shape
    half = yg.shape[1]
    return pl.pallas_call(
        _outproj_kernel,
        grid=(t // tm,),
        in_specs=[
            pl.BlockSpec((tm, half), lambda i: (i, 0)),
            pl.BlockSpec((tm, half), lambda i: (i, 0)),
            pl.BlockSpec((tm, d), lambda i: (i, 0)),
            pl.BlockSpec((2 * half, d), lambda i: (0, 0)),
            pl.BlockSpec((1, d), lambda i: (0, 0)),
        ],
        out_specs=[pl.BlockSpec((tm, d), lambda i: (i, 0)), pl.BlockSpec((tm, d), lambda i: (i, 0))],
        out_shape=[jax.ShapeDtypeStruct((t, d), F32), jax.ShapeDtypeStruct((t, d), BF16)],
        compiler_params=pltpu.CompilerParams(
            dimension_semantics=("parallel",), vmem_limit_bytes=VMEM_LIMIT),
        name="outproj",
    )(yg, ym, x2, wo, nf)


def _topk_rows(sc, n_rows):
    iota = lax.broadcasted_iota(I32, sc.shape, 0)
    vals, idxs = [], []
    for _ in range(PEER_TOPK):
        m = jnp.max(sc, axis=0, keepdims=True)
        am = jnp.min(jnp.where(sc == m, iota, n_rows), axis=0, keepdims=True)
        vals.append(m)
        idxs.append(am)
        sc = jnp.where(iota == am, -jnp.inf, sc)
    return jnp.concatenate(vals, axis=0), jnp.concatenate(idxs, axis=0)


def _route_kernel(xn_ref, wq_ref, keys_ref, idx_ref, g_ref, q_ref):
    qfull = jnp.dot(xn_ref[...], wq_ref[...], preferred_element_type=F32).astype(BF16)
    for hp in range(2 * PEER_HEADS):
        q_ref[hp] = qfull[:, hp * 128:(hp + 1) * 128]

    def head(h, carry):
        sv0, si0 = _topk_rows(_nt(keys_ref[2 * h], q_ref[2 * h]), N_KEYS)
        sv1, si1 = _topk_rows(_nt(keys_ref[2 * h + 1], q_ref[2 * h + 1]), N_KEYS)
        cand = jnp.concatenate([sv0[a:a + 1, :] + sv1 for a in range(PEER_TOPK)], axis=0)
        top_s, pos = _topk_rows(cand, PEER_TOPK * PEER_TOPK)
        pa = pos >> 4
        pb = pos & (PEER_TOPK - 1)
        i0 = jnp.zeros_like(pos)
        i1 = jnp.zeros_like(pos)
        for a in range(PEER_TOPK):
            i0 = i0 + jnp.where(pa == a, si0[a:a + 1, :], 0)
            i1 = i1 + jnp.where(pb == a, si1[a:a + 1, :], 0)
        e = jnp.exp(top_s - top_s[0:1, :])
        r0 = pl.multiple_of(h * PEER_TOPK, PEER_TOPK)
        idx_ref[pl.ds(r0, PEER_TOPK), :] = i0 * N_KEYS + i1
        g_ref[pl.ds(r0, PEER_TOPK), :] = e / jnp.sum(e, axis=0, keepdims=True)
        return carry

    lax.fori_loop(0, PEER_HEADS, head, 0)


def _route(xn2, wq, keys, tb=256):
    t, d = xn2.shape
    return pl.pallas_call(
        _route_kernel,
        grid=(t // tb,),
        in_specs=[
            pl.BlockSpec((tb, d), lambda i: (i, 0)),
            pl.BlockSpec((d, d), lambda i: (0, 0)),
            pl.BlockSpec((2 * PEER_HEADS, N_KEYS, 128), lambda i: (0, 0, 0)),
        ],
        out_specs=[pl.BlockSpec((PEER_PAIRS, tb), lambda i: (0, i)),
                   pl.BlockSpec((PEER_PAIRS, tb), lambda i: (0, i))],
        out_shape=[jax.ShapeDtypeStruct((PEER_PAIRS, t), I32),
                   jax.ShapeDtypeStruct((PEER_PAIRS, t), F32)],
        scratch_shapes=[pltpu.VMEM((2 * PEER_HEADS, tb, 128), BF16)],
        compiler_params=pltpu.CompilerParams(
            dimension_semantics=("parallel",), vmem_limit_bytes=VMEM_LIMIT),
        name="route",
    )(xn2, wq, keys)


EXP_SLOTS = 8
EXP_GROUP = 16
DMA_UNROLL = 16


def _gelu_tanh(x):
    return 0.5 * x * (1.0 + jnp.tanh(0.7978845608028654 * (x + 0.044715 * x * x * x)))


def _experts_kernel(idx_ref, x_ref, g_ref, tab_ref, o_ref, buf_ref, sem_ref):
    tb = x_ref.shape[0]

    def row_copy(e, slot, p):
        return pltpu.make_async_copy(tab_ref.at[pl.ds(e, 1)], buf_ref.at[slot, pl.ds(p, 1)],
                                     sem_ref.at[slot])

    def issue(t, slot):
        def grp(gi, carry):
            for u in range(DMA_UNROLL):
                p = gi * DMA_UNROLL + u
                row_copy(idx_ref[p, t], slot, p).start()
            return carry
        lax.fori_loop(0, PEER_PAIRS // DMA_UNROLL, grp, 0)

    def wait(slot):
        pltpu.make_async_copy(tab_ref.at[pl.ds(0, PEER_PAIRS)], buf_ref.at[slot],
                              sem_ref.at[slot]).wait()

    for s in range(EXP_SLOTS):
        issue(s, s)

    rid = lax.broadcasted_iota(I32, (EXP_GROUP, 2 * PEER_PAIRS), 0)

    def group(gi, carry):
        g0 = pl.multiple_of(gi * EXP_GROUP, EXP_GROUP)
        x16 = x_ref[pl.ds(g0, EXP_GROUP), :]
        g16 = g_ref[pl.ds(g0, EXP_GROUP), :]
        o_ref[pl.ds(g0, EXP_GROUP), :] = jnp.zeros((EXP_GROUP, o_ref.shape[1]), F32)

        def token(r, carry2):
            t = g0 + r
            slot = t % EXP_SLOTS
            wait(slot)
            w = pltpu.bitcast(buf_ref[slot], BF16)
            dots = _nt(x16, w)
            coef = jnp.where(rid == r, _gelu_tanh(dots) * g16, 0.0)
            coef = pltpu.roll(coef, 1, axis=1).astype(BF16)
            o_ref[pl.ds(g0, EXP_GROUP), :] += jnp.dot(coef, w, preferred_element_type=F32)

            @pl.when(t + EXP_SLOTS < tb)
            def _():
                issue(t + EXP_SLOTS, slot)
            return carry2

        lax.fori_loop(0, EXP_GROUP, token, 0)
        return carry

    lax.fori_loop(0, tb // EXP_GROUP, group, 0)


def _experts(idx_t, xn2, g_even, table, tb=128):
    t, d = xn2.shape
    return pl.pallas_call(
        _experts_kernel,
        grid=(t // tb,),
        in_specs=[
            pl.BlockSpec((PEER_PAIRS, tb), lambda i: (0, i), memory_space=pltpu.SMEM),
            pl.BlockSpec((tb, d), lambda i: (i, 0)),
            pl.BlockSpec((tb, 2 * PEER_PAIRS), lambda i: (i, 0)),
            pl.BlockSpec(memory_space=pl.ANY),
        ],
        out_specs=pl.BlockSpec((tb, d), lambda i: (i, 0)),
        out_shape=jax.ShapeDtypeStruct((t, d), F32),
        scratch_shapes=[pltpu.VMEM((EXP_SLOTS, PEER_PAIRS, d), jnp.uint32),
                        pltpu.SemaphoreType.DMA((EXP_SLOTS,))],
        compiler_params=pltpu.CompilerParams(
            dimension_semantics=("arbitrary",), vmem_limit_bytes=VMEM_LIMIT),
        name="experts",
    )(idx_t, xn2, g_even, table)


def _ple_kernel(h_ref, pe_ref, p_ref, wg_ref, wp_ref, np_ref, nfin_ref, o_ref):
    hh = h_ref[...] + pe_ref[...]
    xg = _rms(hh, np_ref[...]).astype(BF16)
    gate = _sigmoid(jnp.dot(xg, wg_ref[...], preferred_element_type=F32))
    pp = jnp.dot(p_ref[...].astype(BF16), wp_ref[...], preferred_element_type=F32)
    o_ref[...] = _rms(hh + gate * pp, nfin_ref[...])


def _ple(h1, pe, p2, wg, wp, npl, nfin, tm=512):
    t, d = h1.shape
    pd = p2.shape[1]
    return pl.pallas_call(
        _ple_kernel,
        grid=(t // tm,),
        in_specs=[
            pl.BlockSpec((tm, d), lambda i: (i, 0)),
            pl.BlockSpec((tm, d), lambda i: (i, 0)),
            pl.BlockSpec((tm, pd), lambda i: (i, 0)),
            pl.BlockSpec((d, d), lambda i: (0, 0)),
            pl.BlockSpec((pd, d), lambda i: (0, 0)),
            pl.BlockSpec((1, d), lambda i: (0, 0)),
            pl.BlockSpec((1, d), lambda i: (0, 0)),
        ],
        out_specs=pl.BlockSpec((tm, d), lambda i: (i, 0)),
        out_shape=jax.ShapeDtypeStruct((t, d), F32),
        compiler_params=pltpu.CompilerParams(
            dimension_semantics=("parallel",), vmem_limit_bytes=VMEM_LIMIT),
        name="ple",
    )(h1, pe, p2, wg, wp, npl, nfin)


def _layer(h2d, p2d, batch, seq, norm_mix, w_in, gla_gate_up, gla_gate_bias, gla_norm, conv_w,
           conv_b, igate_bias, fgate_bias, mlstm_norm, w_out, norm_ffn, peer_query, sub_keys,
           peer_u, peer_v, norm_ple, w_ple, w_ple_gate, norm_out):
    d = h2d.shape[1]
    widths = (512, 512, 1024, 1024, GATE_RANK, 512, 512, 1024, 1024, HEADS, HEADS)
    offs = np.concatenate([[0], np.cumsum(widths)])
    col = lambda i: w_in[:, offs[i]:offs[i + 1]]
    wbig = jnp.concatenate([col(0), col(1), col(2), col(3), col(5), col(6), col(7), col(8)],
                           axis=1).astype(BF16)
    wsm = jnp.concatenate([col(4), col(9), col(10),
                           jnp.zeros((d, 128 - GATE_RANK - 2 * HEADS), F32)], axis=1).astype(BF16)
    big, sm, smt = _inproj(h2d, norm_mix.reshape(1, d), wbig, wsm, wsm.T)

    mconst, lmask = _gla_consts()
    gu = jnp.zeros((HEADS, 128, DK), F32).at[:, :GATE_RANK, :].set(
        gla_gate_up.reshape(GATE_RANK, HEADS, DK).transpose(1, 0, 2)).astype(BF16)
    yg = _gla(big, sm, gu, gla_gate_bias.reshape(HEADS, 1, DK), gla_norm.reshape(HEADS, 1, DV),
              jnp.asarray(mconst, BF16), jnp.asarray(lmask, F32), batch, seq)

    gbc = jnp.zeros((1, 128), F32).at[0, SM_MI:SM_MI + HEADS].set(igate_bias)
    gbc = gbc.at[0, SM_MF:SM_MF + HEADS].set(fgate_bias)
    gbr = jnp.concatenate([igate_bias, fgate_bias]).reshape(8, 1)
    tri = np.tril(np.ones((CHUNK, CHUNK), np.float32))
    ym = _mlstm(big, sm, smt, conv_w, conv_b.reshape(1, -1), gbc, gbr,
                mlstm_norm.reshape(HEADS, 1, DV), jnp.asarray(tri, BF16), jnp.asarray(tri.T, BF16),
                batch, seq)

    h1, xn2 = _outproj(yg, ym, h2d, w_out.astype(BF16), norm_ffn.reshape(1, d))

    keys = sub_keys.reshape(2 * PEER_HEADS, N_KEYS, 128).astype(BF16)
    idx_t, g_t = _route(xn2, peer_query.astype(BF16), keys)

    ub = lax.bitcast_convert_type(peer_u.astype(BF16), jnp.uint16).astype(jnp.uint32)
    vb = lax.bitcast_convert_type(peer_v.astype(BF16), jnp.uint16).astype(jnp.uint32)
    table = ub | (vb << 16)
    g_even = jnp.stack([g_t.T, jnp.zeros_like(g_t.T)], axis=-1).reshape(-1, 2 * PEER_PAIRS)
    pe = _experts(idx_t, xn2, g_even, table)

    return _ple(h1, pe, p2d, w_ple_gate.astype(BF16), w_ple.astype(BF16),
                norm_ple.reshape(1, d), norm_out)


def kernel(x, p, norm_mix, w_in, gla_gate_up, gla_gate_bias, gla_norm, mlstm_conv_w, mlstm_conv_b,
           mlstm_igate_bias, mlstm_fgate_bias, mlstm_norm, w_out, norm_ffn, peer_query,
           peer_sub_keys, peer_u, peer_v, norm_ple, w_ple, w_ple_gate, norm_final):
    batch, seq, d = x.shape
    depth = w_in.shape[0]
    assert depth == 1, "the final norm is fused into the last (only) layer"
    h = x.reshape(batch * seq, d)
    i = 0
    h = _layer(h, p[i].reshape(batch * seq, -1), batch, seq, norm_mix[i], w_in[i], gla_gate_up[i],
               gla_gate_bias[i], gla_norm[i], mlstm_conv_w[i], mlstm_conv_b[i],
               mlstm_igate_bias[i], mlstm_fgate_bias[i], mlstm_norm[i], w_out[i], norm_ffn[i],
               peer_query[i], peer_sub_keys[i], peer_u[i], peer_v[i], norm_ple[i], w_ple[i],
               w_ple_gate[i], norm_final.reshape(1, d))
    return h.reshape(batch, seq, d)
```

```python
import functools

import numpy as np
import jax
import jax.numpy as jnp
from jax import lax
from jax.experimental import pallas as pl
from jax.experimental.pallas import tpu as pltpu

F32 = jnp.float32
BF16 = jnp.bfloat16
I32 = jnp.int32

EPS = 1e-6
CHUNK = 64
GLA_TAU = 16.0
HEADS = 4
DK = 128
DV = 256
QK_SCALE = DK ** -0.5
GATE_RANK = 16
CONV_WIDTH = 4
PEER_HEADS = 8
N_KEYS = 128
PEER_TOPK = 16
PEER_PAIRS = PEER_HEADS * PEER_TOPK

VMEM_LIMIT = 48 * 1024 * 1024

BIG_COLS = 6144
COL_GQ, COL_GK, COL_GV, COL_GG = 0, 512, 1024, 2048
COL_MQ, COL_MK, COL_MV, COL_MO = 3072, 3584, 4096, 5120
SM_GLR, SM_MI, SM_MF = 0, 16, 20


def _log_sigmoid(z):
    return jnp.minimum(z, 0.0) - jnp.log1p(jnp.exp(-jnp.abs(z)))


def _sigmoid(z):
    return 1.0 / (1.0 + jnp.exp(-z))


def _rms(x, g):
    return x * lax.rsqrt(jnp.mean(x * x, axis=-1, keepdims=True) + EPS) * g


def _split3(g):
    g1 = g.astype(BF16)
    r1 = g - g1.astype(F32)
    g2 = r1.astype(BF16)
    r2 = r1 - g2.astype(F32)
    return g1, g2, r2.astype(BF16)


def _nt(a, b):
    return lax.dot_general(a, b, (((1,), (1,)), ((), ())), preferred_element_type=F32)


def _tn(a, b):
    return lax.dot_general(a, b, (((0,), (0,)), ((), ())), preferred_element_type=F32)


def _inproj_kernel(x_ref, nw_ref, wbig_ref, wsm_ref, wsmt_ref, big_ref, sm_ref, smt_ref, xn_ref):
    @pl.when(pl.program_id(1) == 0)
    def _():
        xn = _rms(x_ref[...], nw_ref[...]).astype(BF16)
        xn_ref[...] = xn
        sm_ref[...] = jnp.dot(xn, wsm_ref[...], preferred_element_type=F32)
        smt_ref[...] = _nt(wsmt_ref[...], xn)

    big_ref[...] = jnp.dot(xn_ref[...], wbig_ref[...], preferred_element_type=F32).astype(BF16)


def _inproj(x2, nw, wbig, wsm, wsmt, tm=512, tn=1536):
    t, d = x2.shape
    grid = (t // tm, BIG_COLS // tn)
    return pl.pallas_call(
        _inproj_kernel,
        grid=grid,
        in_specs=[
            pl.BlockSpec((tm, d), lambda i, j: (i, 0)),
            pl.BlockSpec((1, d), lambda i, j: (0, 0)),
            pl.BlockSpec((d, tn), lambda i, j: (0, j)),
            pl.BlockSpec((d, 128), lambda i, j: (0, 0)),
            pl.BlockSpec((128, d), lambda i, j: (0, 0)),
        ],
        out_specs=[
            pl.BlockSpec((tm, tn), lambda i, j: (i, j)),
            pl.BlockSpec((tm, 128), lambda i, j: (i, 0)),
            pl.BlockSpec((128, tm), lambda i, j: (0, i)),
        ],
        out_shape=[
            jax.ShapeDtypeStruct((t, BIG_COLS), BF16),
            jax.ShapeDtypeStruct((t, 128), F32),
            jax.ShapeDtypeStruct((128, t), F32),
        ],
        scratch_shapes=[pltpu.VMEM((tm, d), BF16)],
        compiler_params=pltpu.CompilerParams(
            dimension_semantics=("parallel", "arbitrary"), vmem_limit_bytes=VMEM_LIMIT),
        name="inproj",
    )(x2, nw, wbig, wsm, wsmt)


def _gla_consts():
    c = CHUNK
    r = np.arange(c)
    mats = [(r[None, :] <= r[:, None]), (r[None, :] > r[:, None])]
    masks = []
    for lvl in range(6):
        m = 32 >> lvl
        blk = r // (2 * m)
        second = (r % (2 * m)) >= m
        mid = blk * 2 * m + m - 1
        mq = second[:, None] & (r[None, :] > mid[:, None]) & (r[None, :] <= r[:, None])
        mk = (~second)[:, None] & (r[None, :] > r[:, None]) & (r[None, :] <= mid[:, None])
        mats.append(mq | mk)
        masks.append(second[:, None] & (~second)[None, :] & (blk[:, None] == blk[None, :]))
    return (np.concatenate(mats, 0).astype(np.float32),
            np.stack(masks, 0).astype(np.float32))


def _gla_kernel(q_ref, k_ref, v_ref, gg_ref, sm_ref, gu_ref, gb_ref, gn_ref, mc_ref, lm_ref,
                y_ref, s_ref, la_ref):
    ls = q_ref.shape[0]

    @pl.when(pl.program_id(2) == 0)
    def _():
        s_ref[...] = jnp.zeros_like(s_ref)

    z = jnp.dot(sm_ref[...].astype(BF16), gu_ref[0], preferred_element_type=F32) + gb_ref[0]
    la_ref[...] = _log_sigmoid(z) * (1.0 / GLA_TAU)

    rows = lax.broadcasted_iota(I32, (CHUNK, CHUNK), 0)
    cols = lax.broadcasted_iota(I32, (CHUNK, CHUNK), 1)
    eye = (rows == cols).astype(F32)
    ones_bf = jnp.ones((CHUNK, DK), BF16)

    def chunk(c, carry):
        r0 = pl.multiple_of(c * CHUNK, CHUNK)
        g = la_ref[pl.ds(r0, CHUNK), :]
        g3 = jnp.concatenate(_split3(g), axis=1)
        xx = jnp.dot(mc_ref[...], g3, preferred_element_type=F32)
        xx = xx[:, :DK] + xx[:, DK:2 * DK] + xx[:, 2 * DK:]
        tot = _tn(g3, ones_bf)
        tot = tot[:DK] + tot[DK:2 * DK] + tot[2 * DK:]

        qf = q_ref[pl.ds(r0, CHUNK), :].astype(F32) * QK_SCALE
        kf = k_ref[pl.ds(r0, CHUNK), :].astype(F32)
        v = v_ref[pl.ds(r0, CHUNK), :]
        state = s_ref[...]

        o = jnp.dot((qf * jnp.exp(xx[0:CHUNK])).astype(BF16), state.astype(BF16),
                    preferred_element_type=F32)
        sc = jnp.sum(qf * kf, axis=-1, keepdims=True) * eye
        for lvl in range(6):
            f = jnp.exp(xx[128 + CHUNK * lvl:128 + CHUNK * (lvl + 1)])
            sc = sc + _nt((qf * f).astype(BF16), (kf * f).astype(BF16)) * lm_ref[lvl]
        o = o + jnp.dot(sc.astype(BF16), v, preferred_element_type=F32)

        kd = (kf * jnp.exp(xx[CHUNK:2 * CHUNK])).astype(BF16)
        dec = jnp.exp(tot)
        s_ref[...] = state * jnp.concatenate([dec, dec], axis=1) + _tn(kd, v)

        gg = gg_ref[pl.ds(r0, CHUNK), :].astype(F32)
        y = _rms(o, gn_ref[0]) * (gg * _sigmoid(gg))
        y_ref[pl.ds(r0, CHUNK), :] = y.astype(BF16)
        return carry

    lax.fori_loop(0, ls // CHUNK, chunk, 0)


def _gla(big, sm, gu, gb, gn, mconst, lmask, batch, seq, ls=512):
    t = big.shape[0]
    ns = seq // ls
    row = lambda b, h, s: b * ns + s
    return pl.pallas_call(
        _gla_kernel,
        grid=(batch, HEADS, ns),
        in_specs=[
            pl.BlockSpec((ls, DK), lambda b, h, s: (row(b, h, s), COL_GQ // DK + h)),
            pl.BlockSpec((ls, DK), lambda b, h, s: (row(b, h, s), COL_GK // DK + h)),
            pl.BlockSpec((ls, DV), lambda b, h, s: (row(b, h, s), COL_GV // DV + h)),
            pl.BlockSpec((ls, DV), lambda b, h, s: (row(b, h, s), COL_GG // DV + h)),
            pl.BlockSpec((ls, 128), lambda b, h, s: (row(b, h, s), 0)),
            pl.BlockSpec((1, 128, DK), lambda b, h, s: (h, 0, 0)),
            pl.BlockSpec((1, 1, DK), lambda b, h, s: (h, 0, 0)),
            pl.BlockSpec((1, 1, DV), lambda b, h, s: (h, 0, 0)),
            pl.BlockSpec((8 * CHUNK, CHUNK), lambda b, h, s: (0, 0)),
            pl.BlockSpec((6, CHUNK, CHUNK), lambda b, h, s: (0, 0, 0)),
        ],
        out_specs=pl.BlockSpec((ls, DV), lambda b, h, s: (row(b, h, s), h)),
        out_shape=jax.ShapeDtypeStruct((t, HEADS * DV), BF16),
        scratch_shapes=[pltpu.VMEM((DK, DV), F32), pltpu.VMEM((ls, DK), F32)],
        compiler_params=pltpu.CompilerParams(
            dimension_semantics=("parallel", "parallel", "arbitrary"),
            vmem_limit_bytes=VMEM_LIMIT),
        name="gla",
    )(big, big, big, big, sm, gu, gb, gn, mconst, lmask)


def _mlstm_kernel(q_ref, k_ref, v_ref, mo_ref, sm_ref, smt_ref, cwq_ref, cwk_ref, cbq_ref,
                  cbk_ref, gbc_ref, gbr_ref, mn_ref, tri_ref, trit_ref,
                  y_ref, c_ref, m_ref, qx_ref, kx_ref, qc_ref, kc_ref, fc_ref, ft_ref):
    ls = q_ref.shape[0]
    h = pl.program_id(1)

    @pl.when(pl.program_id(2) == 0)
    def _():
        c_ref[...] = jnp.zeros_like(c_ref)
        m_ref[...] = jnp.zeros_like(m_ref)
        qx_ref[0:8, :] = jnp.zeros((8, DK), F32)
        kx_ref[0:8, :] = jnp.zeros((8, DK), F32)

    qx_ref[8:, :] = q_ref[...].astype(F32)
    kx_ref[8:, :] = k_ref[...].astype(F32)

    def conv(x_ref, w_ref, b_ref):
        acc = b_ref[...] + w_ref[0:1, :] * x_ref[5:5 + ls, :]
        for kk in range(1, CONV_WIDTH):
            acc = acc + w_ref[kk:kk + 1, :] * x_ref[5 + kk:5 + kk + ls, :]
        return acc * _sigmoid(acc)

    qc_ref[...] = conv(qx_ref, cwq_ref, cbq_ref)
    kc_ref[...] = conv(kx_ref, cwk_ref, cbk_ref) * QK_SCALE
    qx_ref[0:8, :] = qx_ref[ls:ls + 8, :]
    kx_ref[0:8, :] = kx_ref[ls:ls + 8, :]

    lane = lax.broadcasted_iota(I32, (ls, 128), 1)
    gc = sm_ref[...] + gbc_ref[...]
    fc_ref[...] = jnp.where(lane >= SM_MF, _log_sigmoid(gc), gc)
    sub = lax.broadcasted_iota(I32, (8, ls), 0)
    gr = smt_ref[...] + gbr_ref[...]
    gr = jnp.where(sub >= HEADS, _log_sigmoid(gr), gr)
    for c in range(ls // CHUNK):
        ft_ref[c] = gr[:, c * CHUNK:(c + 1) * CHUNK]

    rows = lax.broadcasted_iota(I32, (CHUNK, CHUNK), 0)
    cols = lax.broadcasted_iota(I32, (CHUNK, CHUNK), 1)
    causal = cols <= rows
    lane_c = lax.broadcasted_iota(I32, (CHUNK, 128), 1)
    sub_c = lax.broadcasted_iota(I32, (8, CHUNK), 0)
    vlane = lax.broadcasted_iota(I32, (CHUNK, 128), 1)
    one_col = jnp.where(vlane == 0, 1.0, 0.0).astype(BF16)

    def pick_col(a, idx):
        return jnp.sum(jnp.where(lane_c == idx, a, 0.0), axis=1, keepdims=True)

    def pick_row(a, idx):
        return jnp.sum(jnp.where(sub_c == idx, a, 0.0), axis=0, keepdims=True)

    def chunk(c, carry):
        r0 = pl.multiple_of(c * CHUNK, CHUNK)
        gcol = fc_ref[pl.ds(r0, CHUNK), :]
        grow = ft_ref[c]
        cum_c = jnp.dot(tri_ref[...], jnp.concatenate(_split3(gcol), axis=1),
                        preferred_element_type=F32)
        cum_c = cum_c[:, :128] + cum_c[:, 128:256] + cum_c[:, 256:]
        g1, g2, g3 = _split3(grow)
        cum_r = (jnp.dot(g1, trit_ref[...], preferred_element_type=F32)
                 + jnp.dot(g2, trit_ref[...], preferred_element_type=F32)
                 + jnp.dot(g3, trit_ref[...], preferred_element_type=F32))
        f_col = pick_col(cum_c, SM_MF + h)
        i_col = pick_col(gcol, SM_MI + h)
        f_row = pick_row(cum_r, HEADS + h)
        i_row = pick_row(grow, h)
        m_prev = m_ref[0:1, 0:1]

        q = qc_ref[pl.ds(r0, CHUNK), :]
        k = kc_ref[pl.ds(r0, CHUNK), :]
        qb = q.astype(BF16)
        vaug = jnp.concatenate([v_ref[pl.ds(r0, CHUNK), :], one_col], axis=1)

        dmat = jnp.where(causal, f_col - f_row + i_row, -jnp.inf)
        inter = f_col + m_prev
        m_row = jnp.maximum(jnp.max(dmat, axis=1, keepdims=True), inter)
        s = _nt(qb, k.astype(BF16)) * jnp.exp(dmat - m_row)
        w_inter = jnp.exp(inter - m_row)
        cst = c_ref[...]
        comb = (jnp.dot(s.astype(BF16), vaug, preferred_element_type=F32)
                + w_inter * jnp.dot(qb, cst.astype(BF16), preferred_element_type=F32))
        num = comb[:, :DV]
        den = comb[:, DV:DV + 1]
        hh = num / jnp.maximum(jnp.abs(den), jnp.exp(-m_row))

        f_last = f_col[CHUNK - 1:CHUNK, :]
        wlog = f_last - f_col + i_col
        m_new = jnp.maximum(f_last + m_prev, jnp.max(wlog, axis=0, keepdims=True))
        wk = jnp.exp(wlog - m_new)
        c_ref[...] = jnp.exp(f_last + m_prev - m_new) * cst + _tn((wk * k).astype(BF16), vaug)
        m_ref[...] = jnp.broadcast_to(m_new, m_ref.shape)

        mo = mo_ref[pl.ds(r0, CHUNK), :].astype(F32)
        y_ref[pl.ds(r0, CHUNK), :] = (_sigmoid(mo) * _rms(hh, mn_ref[0])).astype(BF16)
        return carry

    lax.fori_loop(0, ls // CHUNK, chunk, 0)


def _mlstm(big, sm, smt, cw, cb, gbc, gbr, mn, tri, trit, batch, seq, ls=512):
    t = big.shape[0]
    ns = seq // ls
    row = lambda b, h, s: b * ns + s
    return pl.pallas_call(
        _mlstm_kernel,
        grid=(batch, HEADS, ns),
        in_specs=[
            pl.BlockSpec((ls, DK), lambda b, h, s: (row(b, h, s), COL_MQ // DK + h)),
            pl.BlockSpec((ls, DK), lambda b, h, s: (row(b, h, s), COL_MK // DK + h)),
            pl.BlockSpec((ls, DV), lambda b, h, s: (row(b, h, s), COL_MV // DV + h)),
            pl.BlockSpec((ls, DV), lambda b, h, s: (row(b, h, s), COL_MO // DV + h)),
            pl.BlockSpec((ls, 128), lambda b, h, s: (row(b, h, s), 0)),
            pl.BlockSpec((8, ls), lambda b, h, s: (SM_MI // 8, row(b, h, s))),
            pl.BlockSpec((CONV_WIDTH, DK), lambda b, h, s: (0, h)),
            pl.BlockSpec((CONV_WIDTH, DK), lambda b, h, s: (0, HEADS + h)),
            pl.BlockSpec((1, DK), lambda b, h, s: (0, h)),
            pl.BlockSpec((1, DK), lambda b, h, s: (0, HEADS + h)),
            pl.BlockSpec((1, 128), lambda b, h, s: (0, 0)),
            pl.BlockSpec((8, 1), lambda b, h, s: (0, 0)),
            pl.BlockSpec((1, 1, DV), lambda b, h, s: (h, 0, 0)),
            pl.BlockSpec((CHUNK, CHUNK), lambda b, h, s: (0, 0)),
            pl.BlockSpec((CHUNK, CHUNK), lambda b, h, s: (0, 0)),
        ],
        out_specs=pl.BlockSpec((ls, DV), lambda b, h, s: (row(b, h, s), h)),
        out_shape=jax.ShapeDtypeStruct((t, HEADS * DV), BF16),
        scratch_shapes=[
            pltpu.VMEM((DK, DV + 128), F32),
            pltpu.VMEM((8, 128), F32),
            pltpu.VMEM((ls + 8, DK), F32),
            pltpu.VMEM((ls + 8, DK), F32),
            pltpu.VMEM((ls, DK), F32),
            pltpu.VMEM((ls, DK), F32),
            pltpu.VMEM((ls, 128), F32),
            pltpu.VMEM((ls // CHUNK, 8, CHUNK), F32),
        ],
        compiler_params=pltpu.CompilerParams(
            dimension_semantics=("parallel", "parallel", "arbitrary"),
            vmem_limit_bytes=VMEM_LIMIT),
        name="mlstm",
    )(big, big, big, big, sm, smt, cw, cw, cb, cb, gbc, gbr, mn, tri, trit)


def _outproj_kernel(yg_ref, ym_ref, x_ref, wo_ref, nf_ref, h_ref, xn_ref):
    half = yg_ref.shape[1]
    acc = jnp.dot(yg_ref[...], wo_ref[0:half, :], preferred_element_type=F32)
    acc = acc + jnp.dot(ym_ref[...], wo_ref[half:, :], preferred_element_type=F32)
    hh = x_ref[...] + acc
    h_ref[...] = hh
    xn_ref[...] = _rms(hh, nf_ref[...])


def _outproj(yg, ym, x2, wo, nf, tm=256):
    t, d = x2.shape
    half = yg.shape[1]
    return pl.pallas_call(
        _outproj_kernel,
        grid=(t // tm,),
        in_specs=[
            pl.BlockSpec((tm, half), lambda i: (i, 0)),
            pl.BlockSpec((tm, half), lambda i: (i, 0)),
            pl.BlockSpec((tm, d), lambda i: (i, 0)),
            pl.BlockSpec((2 * half, d), lambda i: (0, 0)),
            pl.BlockSpec((1, d), lambda i: (0, 0)),
        ],
        out_specs=[pl.BlockSpec((tm, d), lambda i: (i, 0)), pl.BlockSpec((tm, d), lambda i: (i, 0))],
        out_shape=[jax.ShapeDtypeStruct((t, d), F32), jax.ShapeDtypeStruct((t, d), F32)],
        compiler_params=pltpu.CompilerParams(
            dimension_semantics=("parallel",), vmem_limit_bytes=VMEM_LIMIT),
        name="outproj",
    )(yg, ym, x2, wo, nf)


def _topk_rows(sc, n_rows):
    iota = lax.broadcasted_iota(I32, sc.shape, 0)
    vals, idxs = [], []
    for _ in range(PEER_TOPK):
        m = jnp.max(sc, axis=0, keepdims=True)
        am = jnp.min(jnp.where(sc == m, iota, n_rows), axis=0, keepdims=True)
        vals.append(m)
        idxs.append(am)
        sc = jnp.where(iota == am, -jnp.inf, sc)
    return jnp.concatenate(vals, axis=0), jnp.concatenate(idxs, axis=0)


def _route_kernel(xn_ref, wq_ref, keys_ref, idx_ref, g_ref, q_ref):
    qfull = jnp.dot(xn_ref[...].astype(BF16), wq_ref[...],
                    preferred_element_type=F32).astype(BF16)
    for hp in range(2 * PEER_HEADS):
        q_ref[hp] = qfull[:, hp * 128:(hp + 1) * 128]

    def head(h, carry):
        sv0, si0 = _topk_rows(_nt(keys_ref[2 * h], q_ref[2 * h]), N_KEYS)
        sv1, si1 = _topk_rows(_nt(keys_ref[2 * h + 1], q_ref[2 * h + 1]), N_KEYS)
        cand = jnp.concatenate([sv0[a:a + 1, :] + sv1 for a in range(PEER_TOPK)], axis=0)
        top_s, pos = _topk_rows(cand, PEER_TOPK * PEER_TOPK)
        pa = pos >> 4
        pb = pos & (PEER_TOPK - 1)
        i0 = jnp.zeros_like(pos)
        i1 = jnp.zeros_like(pos)
        for a in range(PEER_TOPK):
            i0 = i0 + jnp.where(pa == a, si0[a:a + 1, :], 0)
            i1 = i1 + jnp.where(pb == a, si1[a:a + 1, :], 0)
        e = jnp.exp(top_s - top_s[0:1, :])
        r0 = pl.multiple_of(h * PEER_TOPK, PEER_TOPK)
        idx_ref[pl.ds(r0, PEER_TOPK), :] = i0 * N_KEYS + i1
        g_ref[pl.ds(r0, PEER_TOPK), :] = e / jnp.sum(e, axis=0, keepdims=True)
        return carry

    lax.fori_loop(0, PEER_HEADS, head, 0)


def _route(xn2, wq, keys, tb=256):
    t, d = xn2.shape
    return pl.pallas_call(
        _route_kernel,
        grid=(t // tb,),
        in_specs=[
            pl.BlockSpec((tb, d), lambda i: (i, 0)),
            pl.BlockSpec((d, d), lambda i: (0, 0)),
            pl.BlockSpec((2 * PEER_HEADS, N_KEYS, 128), lambda i: (0, 0, 0)),
        ],
        out_specs=[pl.BlockSpec((PEER_PAIRS, tb), lambda i: (0, i)),
                   pl.BlockSpec((PEER_PAIRS, tb), lambda i: (0, i))],
        out_shape=[jax.ShapeDtypeStruct((PEER_PAIRS, t), I32),
                   jax.ShapeDtypeStruct((PEER_PAIRS, t), F32)],
        scratch_shapes=[pltpu.VMEM((2 * PEER_HEADS, tb, 128), BF16)],
        compiler_params=pltpu.CompilerParams(
            dimension_semantics=("parallel",), vmem_limit_bytes=VMEM_LIMIT),
        name="route",
    )(xn2, wq, keys)


EXP_SLOTS = 8
EXP_GROUP = 16


def _gelu_tanh(x):
    return 0.5 * x * (1.0 + jnp.tanh(0.7978845608028654 * (x + 0.044715 * x * x * x)))


def _experts_kernel(idx_ref, x_ref, g_ref, tab_ref, o_ref, *scratch):
    bufs, sem_ref = scratch[:EXP_SLOTS], scratch[EXP_SLOTS]
    tb = x_ref.shape[0]
    n_iter = tb // EXP_SLOTS

    d = x_ref.shape[1]

    def issue(t, slot):
        for p in range(PEER_PAIRS):
            pltpu.make_async_copy(tab_ref.at[idx_ref[t, p]], bufs[slot].at[pl.ds(p, 1)],
                                  sem_ref.at[slot]).start(priority=p % 2)

    def wait(slot):
        pltpu.make_async_copy(bufs[slot], bufs[slot], sem_ref.at[slot]).wait()

    def group_rows(t):
        return pl.ds(pl.multiple_of((t // EXP_GROUP) * EXP_GROUP, EXP_GROUP), EXP_GROUP)

    rid = lax.broadcasted_iota(I32, (EXP_GROUP, 2 * PEER_PAIRS), 0)
    ones = jnp.ones((EXP_GROUP, 128), BF16)
    high_half = jnp.uint32(0xFFFF0000)

    def stage_a(t, slot):
        xb = jnp.broadcast_to(x_ref[pl.ds(t, 1), :], (8, d))
        parts = []
        for r8 in range(PEER_PAIRS // 8):
            acc = None
            for c in range(d // 128):
                w = bufs[slot][r8 * 8:(r8 + 1) * 8, c * 128:(c + 1) * 128]
                u = pltpu.bitcast(w << 16, F32)
                term = u * xb[:, c * 128:(c + 1) * 128]
                acc = term if acc is None else acc + term
            parts.append(acc)
        r = jnp.concatenate(parts, axis=0)
        hi = pltpu.bitcast(r, jnp.uint32) & high_half
        lo = pltpu.bitcast(r - pltpu.bitcast(hi, F32), jnp.uint32) & high_half
        dots = _nt(ones, pltpu.bitcast(hi, BF16)) + _nt(ones, pltpu.bitcast(lo, BF16))
        coef = jnp.where(rid == t % EXP_GROUP, _gelu_tanh(dots) * g_ref[group_rows(t), :], 0.0)
        return coef.astype(BF16)

    def stage_b(t, slot, coef):
        w = pltpu.bitcast(bufs[slot][...], BF16)
        o_ref[group_rows(t), :] += jnp.dot(coef, w, preferred_element_type=F32)

    @pl.when(pl.program_id(0) == 0)
    def _():
        bufs[EXP_SLOTS - 1][...] = jnp.zeros_like(bufs[EXP_SLOTS - 1])

    o_ref[...] = jnp.zeros_like(o_ref)
    for s in range(EXP_SLOTS - 2):
        issue(s, s)

    def iteration(j, coef, last):
        for s in range(EXP_SLOTS):
            t = j * EXP_SLOTS + s
            wait(s)
            if not last or s < 2:
                issue(t + EXP_SLOTS - 2, (s - 2) % EXP_SLOTS)
            new = stage_a(t, s)
            stage_b(jnp.maximum(t - 1, 0), (s - 1) % EXP_SLOTS, coef)
            coef = new
        return coef

    coef = jnp.zeros((EXP_GROUP, 2 * PEER_PAIRS), BF16)
    coef = lax.fori_loop(0, n_iter - 1, lambda j, c: iteration(j, c, False), coef)
    coef = iteration(n_iter - 1, coef, True)
    stage_b(tb - 1, EXP_SLOTS - 1, coef)


def _experts(idx, xn2, g_odd, table, tb=128):
    t, d = xn2.shape
    return pl.pallas_call(
        _experts_kernel,
        grid=(t // tb,),
        in_specs=[
            pl.BlockSpec((tb, PEER_PAIRS), lambda i: (i, 0), memory_space=pltpu.SMEM),
            pl.BlockSpec((tb, d), lambda i: (i, 0)),
            pl.BlockSpec((tb, 2 * PEER_PAIRS), lambda i: (i, 0)),
            pl.BlockSpec(memory_space=pl.ANY),
        ],
        out_specs=pl.BlockSpec((tb, d), lambda i: (i, 0)),
        out_shape=jax.ShapeDtypeStruct((t, d), F32),
        scratch_shapes=[pltpu.VMEM((PEER_PAIRS, d), jnp.uint32) for _ in range(EXP_SLOTS)]
        + [pltpu.SemaphoreType.DMA((EXP_SLOTS,))],
        compiler_params=pltpu.CompilerParams(
            dimension_semantics=("arbitrary",), vmem_limit_bytes=VMEM_LIMIT),
        name="experts",
    )(idx, xn2, g_odd, table)


def _ple_kernel(h_ref, pe_ref, p_ref, wg_ref, wp_ref, np_ref, nfin_ref, o_ref):
    hh = h_ref[...] + pe_ref[...]
    xg = _rms(hh, np_ref[...]).astype(BF16)
    gate = _sigmoid(jnp.dot(xg, wg_ref[...], preferred_element_type=F32))
    pp = jnp.dot(p_ref[...].astype(BF16), wp_ref[...], preferred_element_type=F32)
    o_ref[...] = _rms(hh + gate * pp, nfin_ref[...])


def _ple(h1, pe, p2, wg, wp, npl, nfin, tm=512):
    t, d = h1.shape
    pd = p2.shape[1]
    return pl.pallas_call(
        _ple_kernel,
        grid=(t // tm,),
        in_specs=[
            pl.BlockSpec((tm, d), lambda i: (i, 0)),
            pl.BlockSpec((tm, d), lambda i: (i, 0)),
            pl.BlockSpec((tm, pd), lambda i: (i, 0)),
            pl.BlockSpec((d, d), lambda i: (0, 0)),
            pl.BlockSpec((pd, d), lambda i: (0, 0)),
            pl.BlockSpec((1, d), lambda i: (0, 0)),
            pl.BlockSpec((1, d), lambda i: (0, 0)),
        ],
        out_specs=pl.BlockSpec((tm, d), lambda i: (i, 0)),
        out_shape=jax.ShapeDtypeStruct((t, d), F32),
        compiler_params=pltpu.CompilerParams(
            dimension_semantics=("parallel",), vmem_limit_bytes=VMEM_LIMIT),
        name="ple",
    )(h1, pe, p2, wg, wp, npl, nfin)


def _layer(h2d, p2d, batch, seq, norm_mix, w_in, gla_gate_up, gla_gate_bias, gla_norm, conv_w,
           conv_b, igate_bias, fgate_bias, mlstm_norm, w_out, norm_ffn, peer_query, sub_keys,
           peer_u, peer_v, norm_ple, w_ple, w_ple_gate, norm_out):
    d = h2d.shape[1]
    widths = (512, 512, 1024, 1024, GATE_RANK, 512, 512, 1024, 1024, HEADS, HEADS)
    offs = np.concatenate([[0], np.cumsum(widths)])
    col = lambda i: w_in[:, offs[i]:offs[i + 1]]
    wbig = jnp.concatenate([col(0), col(1), col(2), col(3), col(5), col(6), col(7), col(8)],
                           axis=1).astype(BF16)
    wsm = jnp.concatenate([col(4), col(9), col(10),
                           jnp.zeros((d, 128 - GATE_RANK - 2 * HEADS), F32)], axis=1).astype(BF16)
    big, sm, smt = _inproj(h2d, norm_mix.reshape(1, d), wbig, wsm, wsm.T)

    mconst, lmask = _gla_consts()
    gu = jnp.zeros((HEADS, 128, DK), F32).at[:, :GATE_RANK, :].set(
        gla_gate_up.reshape(GATE_RANK, HEADS, DK).transpose(1, 0, 2)).astype(BF16)
    yg = _gla(big, sm, gu, gla_gate_bias.reshape(HEADS, 1, DK), gla_norm.reshape(HEADS, 1, DV),
              jnp.asarray(mconst, BF16), jnp.asarray(lmask, F32), batch, seq)

    gbc = jnp.zeros((1, 128), F32).at[0, SM_MI:SM_MI + HEADS].set(igate_bias)
    gbc = gbc.at[0, SM_MF:SM_MF + HEADS].set(fgate_bias)
    gbr = jnp.concatenate([igate_bias, fgate_bias]).reshape(8, 1)
    tri = np.tril(np.ones((CHUNK, CHUNK), np.float32))
    ym = _mlstm(big, sm, smt, conv_w, conv_b.reshape(1, -1), gbc, gbr,
                mlstm_norm.reshape(HEADS, 1, DV), jnp.asarray(tri, BF16), jnp.asarray(tri.T, BF16),
                batch, seq)

    h1, xn2 = _outproj(yg, ym, h2d, w_out.astype(BF16), norm_ffn.reshape(1, d))

    keys = sub_keys.reshape(2 * PEER_HEADS, N_KEYS, 128).astype(BF16)
    idx_t, g_t = _route(xn2, peer_query.astype(BF16), keys)

    ub = lax.bitcast_convert_type(peer_u.astype(BF16), jnp.uint16).astype(jnp.uint32)
    vb = lax.bitcast_convert_type(peer_v.astype(BF16), jnp.uint16).astype(jnp.uint32)
    table = (ub | (vb << 16)).reshape(-1, 1, d)
    g_odd = jnp.stack([jnp.zeros_like(g_t.T), g_t.T], axis=-1).reshape(-1, 2 * PEER_PAIRS)
    pe = _experts(idx_t.T, xn2, g_odd, table)

    return _ple(h1, pe, p2d, w_ple_gate.astype(BF16), w_ple.astype(BF16),
                norm_ple.reshape(1, d), norm_out)


def kernel(x, p, norm_mix, w_in, gla_gate_up, gla_gate_bias, gla_norm, mlstm_conv_w, mlstm_conv_b,
           mlstm_igate_bias, mlstm_fgate_bias, mlstm_norm, w_out, norm_ffn, peer_query,
           peer_sub_keys, peer_u, peer_v, norm_ple, w_ple, w_ple_gate, norm_final):
    batch, seq, d = x.shape
    depth = w_in.shape[0]
    assert depth == 1, "the final norm is fused into the last (only) layer"
    h = x.reshape(batch * seq, d)
    i = 0
    h = _layer(h, p[i].reshape(batch * seq, -1), batch, seq, norm_mix[i], w_in[i], gla_gate_up[i],
               gla_gate_bias[i], gla_norm[i], mlstm_conv_w[i], mlstm_conv_b[i],
               mlstm_igate_bias[i], mlstm_fgate_bias[i], mlstm_norm[i], w_out[i], norm_ffn[i],
               peer_query[i], peer_sub_keys[i], peer_u[i], peer_v[i], norm_ple[i], w_ple[i],
               w_ple_gate[i], norm_final.reshape(1, d))
    return h.reshape(batch, seq, d)
```
